```python
import jax, jax.numpy as jnp
from jax import lax
import numpy as np

D_MODEL = 1024
BATCH = 16
SEQ = 4096
DEPTH = 1
DEC_BATCH = 8
DEC_SEQ = 32
PAST_LEN = 4096

CHUNK = 64
D_MIX = D_MODEL
D_A = D_MIX // 2
D_B = D_MIX - D_A
CONV_A = 3
CONV_B = 31
RMS_EPS = 1e-6
LN_EPS = 1e-5
D_IN = 4 * D_A + 3 * D_B
SPLITS = [D_A, 2 * D_A, 3 * D_A, 4 * D_A, 4 * D_A + D_B, 4 * D_A + 2 * D_B]

kernel_name = "hybrid_shortconv_conformerconv_stream_step"


def rms_norm(x, g):
    x32 = x.astype(jnp.float32)
    y = x32 * lax.rsqrt(jnp.mean(x32 * x32, axis=-1, keepdims=True) + RMS_EPS)
    return (y * g.astype(jnp.float32)).astype(x.dtype)


def layer_norm(x, g, b):
    x32 = x.astype(jnp.float32)
    mu = jnp.mean(x32, axis=-1, keepdims=True)
    xc = x32 - mu
    var = jnp.mean(xc * xc, axis=-1, keepdims=True)
    y = xc * lax.rsqrt(var + LN_EPS) * g.astype(jnp.float32) + b.astype(jnp.float32)
    return y.astype(x.dtype)


def causal_dwconv(hist, u, w):
    k1 = w.shape[0] - 1
    xp = jnp.concatenate([hist.astype(u.dtype), u], axis=1)
    y = lax.conv_general_dilated(
        xp, w[:, None, :].astype(u.dtype), window_strides=(1,), padding='VALID',
        dimension_numbers=('NWC', 'WIO', 'NWC'), feature_group_count=u.shape[-1])
    return y, xp[:, xp.shape[1] - k1:, :]


def mixer_layer(x, hist_a, hist_b, norm_g, w_in, conv_a_w, conv_b_w, conv_b_bias,
                ln_b_g, ln_b_b, w_out):
    hn = rms_norm(x, norm_g)
    proj = jnp.einsum('btd,de->bte', hn, w_in.astype(hn.dtype))
    h_a, b_a, c_a, z_a, a_b, g_b, z_b = jnp.split(proj, SPLITS, axis=-1)
    conv_a, new_a = causal_dwconv(hist_a, c_a * h_a, conv_a_w)
    out_a = b_a * conv_a * jax.nn.silu(z_a)
    v_b = a_b * jax.nn.sigmoid(g_b)
    conv_b, new_b = causal_dwconv(hist_b, v_b, conv_b_w)
    s_b = jax.nn.silu(layer_norm(conv_b + conv_b_bias.astype(conv_b.dtype), ln_b_g, ln_b_b))
    out_b = s_b * jax.nn.silu(z_b)
    mixed = jnp.concatenate([out_a, out_b], axis=-1)
    y = jnp.einsum('bte,ed->btd', mixed, w_out.astype(mixed.dtype))
    return x + y, new_a, new_b


def setup_inputs(seed: int = 0) -> dict:
    key = jax.random.key(seed)
    ks = jax.random.split(key, 14)
    f32 = jnp.float32
    return {
        "x_prompt": jax.random.normal(ks[0], (BATCH, SEQ, D_MODEL), f32),
        "x_sample": jax.random.normal(ks[1], (DEC_BATCH, DEC_SEQ, D_MODEL), f32),
        "state_conv_a": jax.random.normal(ks[2], (DEPTH, DEC_BATCH, CONV_A - 1, D_A), f32),
        "state_conv_b": 0.5 * jax.random.normal(ks[3], (DEPTH, DEC_BATCH, CONV_B - 1, D_B), f32),
        "norm_g": 1.0 + 0.02 * jax.random.normal(ks[4], (DEPTH, D_MODEL), f32),
        "w_in": jax.random.normal(ks[5], (DEPTH, D_MODEL, D_IN), f32) * D_MODEL ** -0.5,
        "conv_a_w": jax.random.normal(ks[6], (DEPTH, CONV_A, D_A), f32) * CONV_A ** -0.5,
        "conv_b_w": jax.random.normal(ks[7], (DEPTH, CONV_B, D_B), f32) * CONV_B ** -0.5,
        "conv_b_bias": 0.02 * jax.random.normal(ks[8], (DEPTH, D_B), f32),
        "ln_b_g": 1.0 + 0.02 * jax.random.normal(ks[9], (DEPTH, D_B), f32),
        "ln_b_b": 0.02 * jax.random.normal(ks[10], (DEPTH, D_B), f32),
        "w_out": jax.random.normal(ks[11], (DEPTH, D_MIX, D_MODEL), f32) * D_MIX ** -0.5,
        "final_norm_g": 1.0 + 0.02 * jax.random.normal(ks[12], (D_MODEL,), f32),
    }


def reference(x_prompt, x_sample, state_conv_a, state_conv_b, norm_g, w_in, conv_a_w,
              conv_b_w, conv_b_bias, ln_b_g, ln_b_b, w_out, final_norm_g):
    hp, hs = x_prompt, x_sample
    bp = x_prompt.shape[0]
    pa_list, pb_list, sa_list, sb_list = [], [], [], []
    for l in range(DEPTH):
        zero_a = jnp.zeros((bp, CONV_A - 1, D_A), hp.dtype)
        zero_b = jnp.zeros((bp, CONV_B - 1, D_B), hp.dtype)
        hp, pa, pb = mixer_layer(hp, zero_a, zero_b, norm_g[l], w_in[l], conv_a_w[l],
                                 conv_b_w[l], conv_b_bias[l], ln_b_g[l], ln_b_b[l], w_out[l])
        hs, sa, sb = mixer_layer(hs, state_conv_a[l], state_conv_b[l], norm_g[l], w_in[l],
                                 conv_a_w[l], conv_b_w[l], conv_b_bias[l], ln_b_g[l],
                                 ln_b_b[l], w_out[l])
        pa_list.append(pa); pb_list.append(pb); sa_list.append(sa); sb_list.append(sb)
    y_prompt = rms_norm(hp, final_norm_g)
    y_sample = rms_norm(hs, final_norm_g)
    new_conv_a_prompt = jnp.stack(pa_list, axis=0)
    new_conv_b_prompt = jnp.stack(pb_list, axis=0)
    new_conv_a_sample = jnp.stack(sa_list, axis=0)
    new_conv_b_sample = jnp.stack(sb_list, axis=0)
    return (y_prompt, y_sample, new_conv_a_prompt, new_conv_b_prompt, new_conv_a_sample, new_conv_b_sample)
```

```python
import functools

import jax
import jax.numpy as jnp
from jax import lax
from jax.experimental import pallas as pl
from jax.experimental.pallas import tpu as pltpu

RMS_EPS = 1e-6
LN_EPS = 1e-5

SUBLANES = 8
LANES = 128
VMEM_LIMIT_BYTES = 56 * 1024 * 1024
SEQ_TILE = 512
ROW_STRIDE = 2


def _sigmoid(x):
    return 1.0 / (1.0 + jnp.exp(-x))


def _silu(x):
    return x * _sigmoid(x)


def _round_up(n, m):
    return (n + m - 1) // m * m


def _time_rows(first, count):
    return pl.ds(ROW_STRIDE * first, count, stride=ROW_STRIDE)


def _mixer_kernel(x_ref, ha_ref, hb_ref, ng_ref, win_ref, caw_ref, cbw_ref, cbb_ref,
                  lng_ref, lnb_ref, wout_ref, fg_ref,
                  y_ref, na_ref, nb_ref,
                  ua_scr, vb_scr, cb_scr, *, tile, d_a, d_b, ka, kb):
    s = pl.program_id(1)
    ns = pl.num_programs(1)
    pad_a = ha_ref.shape[1]
    pad_b = hb_ref.shape[1]

    def lanes(lb):
        return slice(lb * LANES, (lb + 1) * LANES)

    @pl.when(s == 0)
    def _():
        for lb in range(d_a // LANES):
            ua_scr[lb, _time_rows(0, pad_a), :] = ha_ref[0, :, lanes(lb)]
        for lb in range(d_b // LANES):
            vb_scr[lb, _time_rows(0, pad_b), :] = hb_ref[0, :, lanes(lb)]

    x = x_ref[0]
    ms = jnp.mean(x * x, axis=-1, keepdims=True)
    hn = (x * lax.rsqrt(ms + RMS_EPS) * ng_ref[...]).astype(jnp.bfloat16)

    def proj(col0, width):
        return jnp.dot(hn, win_ref[:, col0:col0 + width],
                       preferred_element_type=jnp.float32)

    u = proj(2 * d_a, d_a) * proj(0, d_a)
    conv_a = []
    for lb in range(d_a // LANES):
        ua_scr[lb, _time_rows(pad_a, tile), :] = u[:, lanes(lb)]
        acc = caw_ref[ka - 1:ka, lanes(lb)] * u[:, lanes(lb)]
        for k in range(ka - 1):
            acc = acc + (caw_ref[k:k + 1, lanes(lb)]
                         * ua_scr[lb, _time_rows(pad_a - (ka - 1) + k, tile), :])
        conv_a.append(acc)
    conv_a = jnp.concatenate(conv_a, axis=-1)
    out_a = proj(d_a, d_a) * conv_a * _silu(proj(3 * d_a, d_a))

    v = proj(4 * d_a, d_b) * _sigmoid(proj(4 * d_a + d_b, d_b))

    n_acc = min(8, tile // SUBLANES)
    chunk = n_acc * SUBLANES
    for lb in range(d_b // LANES):
        vb_scr[lb, _time_rows(pad_b, tile), :] = v[:, lanes(lb)]
        wk = [jnp.broadcast_to(cbw_ref[k:k + 1, lanes(lb)], (SUBLANES, LANES))
              for k in range(kb)]
        bias = jnp.broadcast_to(cbb_ref[:, lanes(lb)], (SUBLANES, LANES))
        for r0 in range(0, tile, chunk):
            accs = [bias] * n_acc
            first = pad_b - (kb - 1)
            for off in range(SUBLANES * (n_acc - 1) + kb):
                d = vb_scr[lb, _time_rows(r0 + first + off, SUBLANES), :]
                for a in range(n_acc):
                    k = off - SUBLANES * a
                    if 0 <= k < kb:
                        accs[a] = accs[a] + wk[k] * d
            for a in range(n_acc):
                cb_scr[r0 + SUBLANES * a:r0 + SUBLANES * (a + 1), lanes(lb)] = accs[a]

    cb = cb_scr[...]
    mu = jnp.mean(cb, axis=-1, keepdims=True)
    xc = cb - mu
    var = jnp.mean(xc * xc, axis=-1, keepdims=True)
    ln = xc * lax.rsqrt(var + LN_EPS) * lng_ref[...] + lnb_ref[...]
    out_b = _silu(ln) * _silu(proj(4 * d_a + 2 * d_b, d_b))

    y = (jnp.dot(out_a.astype(jnp.bfloat16), wout_ref[0:d_a, :],
                 preferred_element_type=jnp.float32)
         + jnp.dot(out_b.astype(jnp.bfloat16), wout_ref[d_a:d_a + d_b, :],
                   preferred_element_type=jnp.float32))
    h = x + y
    ms2 = jnp.mean(h * h, axis=-1, keepdims=True)
    y_ref[0] = h * lax.rsqrt(ms2 + RMS_EPS) * fg_ref[...]

    new_a = u[tile - pad_a:tile, :]
    new_b = v[tile - pad_b:tile, :]

    @pl.when(s == ns - 1)
    def _():
        na_ref[0] = new_a
        nb_ref[0] = new_b

    @pl.when(s < ns - 1)
    def _():
        for lb in range(d_a // LANES):
            ua_scr[lb, _time_rows(0, pad_a), :] = new_a[:, lanes(lb)]
        for lb in range(d_b // LANES):
            vb_scr[lb, _time_rows(0, pad_b), :] = new_b[:, lanes(lb)]


def _mixer_step(x, hist_a, hist_b, norm_g, w_in_bf16, conv_a_w, conv_b_w, conv_b_bias,
                ln_g, ln_b, w_out_bf16, final_g):
    bsz, seq, d_model = x.shape
    ka, d_a = conv_a_w.shape
    kb, d_b = conv_b_w.shape
    d_in = w_in_bf16.shape[1]
    tile = min(SEQ_TILE, seq)
    pad_a = _round_up(ka - 1, SUBLANES)
    pad_b = _round_up(kb - 1, SUBLANES)
    assert seq % tile == 0 and tile % SUBLANES == 0 and tile >= pad_b
    assert d_a % LANES == 0 and d_b % LANES == 0 and d_in == 4 * d_a + 3 * d_b

    hist_a = jnp.pad(hist_a, ((0, 0), (pad_a - (ka - 1), 0), (0, 0)))
    hist_b = jnp.pad(hist_b, ((0, 0), (pad_b - (kb - 1), 0), (0, 0)))

    const2 = lambda b, s: (0, 0)
    per_seq = lambda b, s: (b, 0, 0)
    per_tile = lambda b, s: (b, s, 0)
    row = lambda a: a.reshape(1, -1)

    kern = functools.partial(_mixer_kernel, tile=tile, d_a=d_a, d_b=d_b, ka=ka, kb=kb)
    y, new_a, new_b = pl.pallas_call(
        kern,
        grid=(bsz, seq // tile),
        in_specs=[
            pl.BlockSpec((1, tile, d_model), per_tile),
            pl.BlockSpec((1, pad_a, d_a), per_seq),
            pl.BlockSpec((1, pad_b, d_b), per_seq),
            pl.BlockSpec((1, d_model), const2),
            pl.BlockSpec((d_model, d_in), const2),
            pl.BlockSpec((ka, d_a), const2),
            pl.BlockSpec((kb, d_b), const2),
            pl.BlockSpec((1, d_b), const2),
            pl.BlockSpec((1, d_b), const2),
            pl.BlockSpec((1, d_b), const2),
            pl.BlockSpec((d_a + d_b, d_model), const2),
            pl.BlockSpec((1, d_model), const2),
        ],
        out_specs=[
            pl.BlockSpec((1, tile, d_model), per_tile),
            pl.BlockSpec((1, pad_a, d_a), per_seq),
            pl.BlockSpec((1, pad_b, d_b), per_seq),
        ],
        out_shape=[
            jax.ShapeDtypeStruct((bsz, seq, d_model), x.dtype),
            jax.ShapeDtypeStruct((bsz, pad_a, d_a), x.dtype),
            jax.ShapeDtypeStruct((bsz, pad_b, d_b), x.dtype),
        ],
        scratch_shapes=[
            pltpu.VMEM((d_a // LANES, ROW_STRIDE * (pad_a + tile), LANES), jnp.float32),
            pltpu.VMEM((d_b // LANES, ROW_STRIDE * (pad_b + tile), LANES), jnp.float32),
            pltpu.VMEM((tile, d_b), jnp.float32),
        ],
        compiler_params=pltpu.CompilerParams(
            dimension_semantics=("arbitrary", "arbitrary"),
            vmem_limit_bytes=VMEM_LIMIT_BYTES),
        name=f"mixer_step_t{tile}",
    )(x, hist_a, hist_b, row(norm_g), w_in_bf16, conv_a_w, conv_b_w, row(conv_b_bias),
      row(ln_g), row(ln_b), w_out_bf16, row(final_g))
    return y, new_a[:, pad_a - (ka - 1):], new_b[:, pad_b - (kb - 1):]


def kernel(x_prompt, x_sample, state_conv_a, state_conv_b, norm_g, w_in, conv_a_w, conv_b_w,
           conv_b_bias, ln_b_g, ln_b_b, w_out, final_norm_g):
    assert norm_g.shape[0] == 1
    bp = x_prompt.shape[0]
    weights = (norm_g[0], w_in[0].astype(jnp.bfloat16), conv_a_w[0], conv_b_w[0], conv_b_bias[0],
               ln_b_g[0], ln_b_b[0], w_out[0].astype(jnp.bfloat16), final_norm_g)
    zero_a = jnp.zeros((bp,) + state_conv_a.shape[2:], x_prompt.dtype)
    zero_b = jnp.zeros((bp,) + state_conv_b.shape[2:], x_prompt.dtype)
    yp, pa, pb = _mixer_step(x_prompt, zero_a, zero_b, *weights)
    ys, sa, sb = _mixer_step(x_sample, state_conv_a[0], state_conv_b[0], *weights)
    return (yp, ys, pa[None], pb[None], sa[None], sb[None])
```
